```python
import math
import jax, jax.numpy as jnp
from jax import lax
import numpy as np

D_MODEL = 1024
BATCH = 2
SEQ = 16384
DEPTH = 1
DEC_BATCH = 2
DEC_SEQ = 8192
PAST_LEN = 128

HEAD_DIM = 64
N_HEADS_A = 8
N_GROUPS_B = 8
WIDTH_A = N_HEADS_A * HEAD_DIM
WIDTH_B = N_GROUPS_B * HEAD_DIM
D_MIX = WIDTH_A + WIDTH_B
IN_COLS = 2 * WIDTH_A + 3 * WIDTH_B
CHUNK = 128
CONV_W = 3
N_EXPERTS = 256
TOP_K = 8
N_EXPERT_GROUPS = 8
TOPK_GROUPS = 4
D_EXPERT = 256
ROUTED_SCALE = 2.5
EXPERT_BLOCK = 128
ALPHA = (2.0 * DEPTH) ** 0.25
BETA = (8.0 * DEPTH) ** -0.25
EPS = 1e-5

kernel_name = "hybrid_gmlp_shortconv_moe_encoder"


def layer_norm(x, g, b):
    xf = x.astype(jnp.float32)
    mu = jnp.mean(xf, axis=-1, keepdims=True)
    var = jnp.mean(jnp.square(xf - mu), axis=-1, keepdims=True)
    y = (xf - mu) * lax.rsqrt(var + EPS) * g.astype(jnp.float32) + b.astype(jnp.float32)
    return y.astype(x.dtype)


def rms_norm(x, g):
    xf = x.astype(jnp.float32)
    y = xf * lax.rsqrt(jnp.mean(jnp.square(xf), axis=-1, keepdims=True) + EPS) * g.astype(jnp.float32)
    return y.astype(x.dtype)


def spatial_gating(u, v, ln_v_g, ln_v_b, w_s, b_s):
    bn, s, _ = v.shape
    v = v.reshape(bn, s // CHUNK, CHUNK, N_HEADS_A, HEAD_DIM)
    v = layer_norm(v, ln_v_g.reshape(N_HEADS_A, HEAD_DIM), ln_v_b.reshape(N_HEADS_A, HEAD_DIM))
    mixed = jnp.einsum('hpq,bnqhd->bnphd', w_s, v) + b_s.T[None, None, :, :, None]
    return u * mixed.reshape(bn, s, WIDTH_A)


def short_conv(x_b, b_gate, c_gate, conv_w):
    s = x_b.shape[1]
    half = CONV_W // 2
    z = jnp.pad(c_gate * x_b, ((0, 0), (half, half), (0, 0)))
    y = z[:, 0:s] * conv_w[:, 0]
    for k in range(1, CONV_W):
        y = y + z[:, k:k + s] * conv_w[:, k]
    return b_gate * y


def token_mixer(x, w_in, ln_v_g, ln_v_b, w_s, b_s, conv_w, g_out_a, g_out_b, w_out):
    proj = x @ w_in
    u, v, b_gate, c_gate, x_b = jnp.split(
        proj, [WIDTH_A, 2 * WIDTH_A, 2 * WIDTH_A + WIDTH_B, 2 * WIDTH_A + 2 * WIDTH_B], axis=-1)
    ya = spatial_gating(jax.nn.gelu(u), jax.nn.gelu(v), ln_v_g, ln_v_b, w_s, b_s)
    yb = short_conv(x_b, b_gate, c_gate, conv_w)
    y = jnp.concatenate([rms_norm(ya, g_out_a), rms_norm(yb, g_out_b)], axis=-1)
    return y @ w_out


def route(h, w_router, router_bias):
    n = h.shape[0]
    s = jax.nn.sigmoid(h.astype(jnp.float32) @ w_router.astype(jnp.float32))
    s_sel = s + router_bias.astype(jnp.float32)
    per_group = N_EXPERTS // N_EXPERT_GROUPS
    grp_score = lax.top_k(s_sel.reshape(n, N_EXPERT_GROUPS, per_group), 2)[0].sum(-1)
    _, gidx = lax.top_k(grp_score, TOPK_GROUPS)
    gmask = jax.nn.one_hot(gidx, N_EXPERT_GROUPS, dtype=jnp.float32).sum(-2) > 0.5
    emask = jnp.repeat(gmask, per_group, axis=-1)
    _, idx = lax.top_k(jnp.where(emask, s_sel, -jnp.inf), TOP_K)
    w = jnp.take_along_axis(s, idx, axis=-1)
    w = w / jnp.sum(w, axis=-1, keepdims=True) * ROUTED_SCALE
    return idx.astype(jnp.int32), w


def routed_experts(hf, idx, gates, w_gate_up, w_down):
    n, d = hf.shape
    nk = n * TOP_K
    nb = -(-nk // EXPERT_BLOCK) + N_EXPERTS
    p = nb * EXPERT_BLOCK
    flat_e = idx.reshape(-1)
    flat_tok = jnp.repeat(jnp.arange(n, dtype=jnp.int32), TOP_K)
    flat_g = gates.reshape(-1)
    order = jnp.argsort(flat_e, stable=True)
    se = flat_e[order]
    counts = jnp.bincount(flat_e, length=N_EXPERTS).astype(jnp.int32)
    padded = (counts + EXPERT_BLOCK - 1) // EXPERT_BLOCK * EXPERT_BLOCK
    start_sorted = jnp.cumsum(counts) - counts
    pad_end = jnp.cumsum(padded)
    start_pad = pad_end - padded
    dest = start_pad[se] + jnp.arange(nk, dtype=jnp.int32) - start_sorted[se]
    slot_tok = jnp.full((p,), n, dtype=jnp.int32).at[dest].set(flat_tok[order])
    slot_gate = jnp.zeros((p,), hf.dtype).at[dest].set(flat_g[order].astype(hf.dtype))
    block_e = jnp.minimum(
        jnp.searchsorted(pad_end, jnp.arange(nb, dtype=jnp.int32) * EXPERT_BLOCK, side='right'),
        N_EXPERTS - 1).astype(jnp.int32)
    h_pad = jnp.concatenate([hf, jnp.zeros((1, d), hf.dtype)], axis=0)

    def step(acc, blk):
        tok, g, e = blk
        xb = h_pad[tok]
        a, b = jnp.split(xb @ w_gate_up[e], 2, axis=-1)
        o = (jax.nn.silu(a) * b) @ w_down[e]
        return acc.at[tok].add(o * g[:, None]), None

    acc, _ = lax.scan(step, jnp.zeros((n + 1, d), hf.dtype),
                      (slot_tok.reshape(nb, EXPERT_BLOCK), slot_gate.reshape(nb, EXPERT_BLOCK), block_e))
    return acc[:n]


def shared_expert(hf, w_gu, w_d):
    a, b = jnp.split(hf @ w_gu, 2, axis=-1)
    return (jax.nn.silu(a) * b) @ w_d


def encoder_layer(x, w_in, ln_v_g, ln_v_b, w_spatial, b_spatial, conv_w, g_out_a, g_out_b, w_out,
                  ln1_g, ln1_b, w_router, router_bias, w_gate_up, w_down,
                  w_shared_gate_up, w_shared_down, ln2_g, ln2_b):
    bn, s, d = x.shape
    mix = token_mixer(x, w_in, ln_v_g, ln_v_b, w_spatial, b_spatial, conv_w, g_out_a, g_out_b, w_out)
    h = layer_norm(ALPHA * x + mix, ln1_g, ln1_b)
    hf = h.reshape(bn * s, d)
    idx, gates = route(hf, w_router, router_bias)
    moe = routed_experts(hf, idx, gates, w_gate_up, w_down) + shared_expert(hf, w_shared_gate_up, w_shared_down)
    y = layer_norm(ALPHA * hf + moe, ln2_g, ln2_b)
    return y.reshape(bn, s, d)


def setup_inputs(seed: int = 0) -> dict:
    key = jax.random.key(seed)
    ks = jax.random.split(key, 24)
    f32 = jnp.float32
    nrm = lambda k, shape, scale: jax.random.normal(k, shape, f32) * scale
    L = DEPTH
    return {
        "x_prompt": jax.random.normal(ks[0], (BATCH, SEQ, D_MODEL), f32),
        "x_sample": jax.random.normal(ks[1], (DEC_BATCH, DEC_SEQ, D_MODEL), f32),
        "w_in": nrm(ks[2], (L, D_MODEL, IN_COLS), D_MODEL ** -0.5),
        "ln_v_g": 1.0 + nrm(ks[3], (L, WIDTH_A), 0.1),
        "ln_v_b": nrm(ks[4], (L, WIDTH_A), 0.1),
        "w_spatial": nrm(ks[5], (L, N_HEADS_A, CHUNK, CHUNK), CHUNK ** -0.5),
        "b_spatial": 1.0 + nrm(ks[6], (L, N_HEADS_A, CHUNK), 0.1),
        "conv_w": nrm(ks[7], (L, WIDTH_B, CONV_W), CONV_W ** -0.5),
        "g_out_a": 1.0 + nrm(ks[8], (L, WIDTH_A), 0.1),
        "g_out_b": 1.0 + nrm(ks[9], (L, WIDTH_B), 0.1),
        "w_out": nrm(ks[10], (L, D_MIX, D_MODEL), BETA * D_MIX ** -0.5),
        "ln1_g": 1.0 + nrm(ks[11], (L, D_MODEL), 0.1),
        "ln1_b": nrm(ks[12], (L, D_MODEL), 0.1),
        "w_router": nrm(ks[13], (L, D_MODEL, N_EXPERTS), D_MODEL ** -0.5),
        "router_bias": nrm(ks[14], (L, N_EXPERTS), 0.01),
        "w_gate_up": nrm(ks[15], (L, N_EXPERTS, D_MODEL, 2 * D_EXPERT), D_MODEL ** -0.5),
        "w_down": nrm(ks[16], (L, N_EXPERTS, D_EXPERT, D_MODEL), BETA * D_EXPERT ** -0.5),
        "w_shared_gate_up": nrm(ks[17], (L, D_MODEL, 2 * D_EXPERT), D_MODEL ** -0.5),
        "w_shared_down": nrm(ks[18], (L, D_EXPERT, D_MODEL), BETA * D_EXPERT ** -0.5),
        "ln2_g": 1.0 + nrm(ks[19], (L, D_MODEL), 0.1),
        "ln2_b": nrm(ks[20], (L, D_MODEL), 0.1),
    }


def reference(x_prompt, x_sample, w_in, ln_v_g, ln_v_b, w_spatial, b_spatial, conv_w, g_out_a, g_out_b,
              w_out, ln1_g, ln1_b, w_router, router_bias, w_gate_up, w_down, w_shared_gate_up,
              w_shared_down, ln2_g, ln2_b):
    def trunk(x):
        for l in range(DEPTH):
            x = encoder_layer(x, w_in[l], ln_v_g[l], ln_v_b[l], w_spatial[l], b_spatial[l], conv_w[l],
                              g_out_a[l], g_out_b[l], w_out[l], ln1_g[l], ln1_b[l], w_router[l],
                              router_bias[l], w_gate_up[l], w_down[l], w_shared_gate_up[l],
                              w_shared_down[l], ln2_g[l], ln2_b[l])
        return x

    y_prompt = trunk(x_prompt)
    y_sample = trunk(x_sample)
    return (y_prompt, y_sample)
```

```python
import functools

import jax
import jax.numpy as jnp
from jax import lax
from jax.experimental import pallas as pl
from jax.experimental.pallas import tpu as pltpu

D_MODEL = 1024
HEAD_DIM = 64
N_HEADS_A = 8
WIDTH_A = 512
WIDTH_B = 512
CHUNK = 128
N_EXPERTS = 256
TOP_K = 8
N_GROUPS = 8
TOPK_GROUPS = 4
PER_GROUP = N_EXPERTS // N_GROUPS
D_EXPERT = 256
ROUTED_SCALE = 2.5
ALPHA = 2.0 ** 0.25
EPS = 1e-5

LANES = 128
HALO = 16
HALF = D_MODEL // 2

MIX_TILE = 512
ROUTE_TILE = 512
FINAL_TILE = 512
EXPERT_ROWS = 256
ISSUE_UNROLL = 8
VMEM_LIMIT = 56 * 1024 * 1024

f32 = jnp.float32
bf16 = jnp.bfloat16
u32 = jnp.uint32
i32 = jnp.int32


def _dot(a, b):
    return jnp.dot(a, b, preferred_element_type=f32)


def _pack_rows(v):
    bits = lax.bitcast_convert_type(v.astype(bf16).astype(f32), u32)
    return bits[:, :HALF] | (bits[:, HALF:] >> 16)


def _unpack_hi(w):
    return lax.bitcast_convert_type(w & jnp.uint32(0xFFFF0000), f32)


def _unpack_lo(w):
    return lax.bitcast_convert_type(w << 16, f32)


def _layer_norm(r, g, b):
    mu = jnp.mean(r, axis=-1, keepdims=True)
    d = r - mu
    var = jnp.mean(d * d, axis=-1, keepdims=True)
    return d * lax.rsqrt(var + EPS) * g + b


def _mixer_kernel(xp_ref, x_ref, xn_ref, w_in_ref, seg_ref, lnv_g_ref, lnv_b_ref, ws_ref, bs_ref,
                  cw_ref, ga_ref, gb_ref, w_out_ref, ln1g_ref, ln1b_ref, h_ref, hp_ref,
                  *, tile, start_tiles, end_tiles):
    i = pl.program_id(0)
    at_start = functools.reduce(jnp.logical_or, [i == s for s in start_tiles])
    at_end = functools.reduce(jnp.logical_or, [i == s for s in end_tiles])

    x = x_ref[...]
    xb = x.astype(bf16)
    proj = _dot(xb, w_in_ref[:, :3 * WIDTH_A])
    xext = jnp.concatenate([xp_ref[...].astype(bf16), xb, xn_ref[...].astype(bf16)], axis=0)
    pconv = _dot(xext, w_in_ref[:, 3 * WIDTH_A:])
    u = jax.nn.gelu(proj[:, :WIDTH_A], approximate=True)
    v = jax.nn.gelu(proj[:, WIDTH_A:2 * WIDTH_A], approximate=True)
    b_gate = proj[:, 2 * WIDTH_A:]

    seg = seg_ref[...]
    mu = _dot(v.astype(bf16), seg) * (1.0 / HEAD_DIM)
    dv = v - mu
    var = _dot((dv * dv).astype(bf16), seg) * (1.0 / HEAD_DIM)
    vn = (dv * lax.rsqrt(var + EPS) * lnv_g_ref[...] + lnv_b_ref[...]).astype(bf16)

    low_lanes = lax.broadcasted_iota(i32, (CHUNK, LANES), 1) < HEAD_DIM
    chunks = []
    for c in range(tile // CHUNK):
        cols = []
        for j in range(WIDTH_A // LANES):
            vj = vn[c * CHUNK:(c + 1) * CHUNK, j * LANES:(j + 1) * LANES]
            cols.append(jnp.where(low_lanes, _dot(ws_ref[2 * j], vj), _dot(ws_ref[2 * j + 1], vj)))
        chunks.append(jnp.concatenate(cols, axis=1) + bs_ref[...])
    ya = u * jnp.concatenate(chunks, axis=0)

    z = pconv[:, :WIDTH_B] * pconv[:, WIDTH_B:]
    rows = lax.broadcasted_iota(i32, (tile + 2 * HALO, 1), 0)
    outside = jnp.logical_or(jnp.logical_and(at_start, rows < HALO),
                             jnp.logical_and(at_end, rows >= tile + HALO))
    z = jnp.where(outside, 0.0, z)
    z_prev = pltpu.roll(z, 1, 0)[HALO:HALO + tile]
    z_next = pltpu.roll(z, tile + 2 * HALO - 1, 0)[HALO:HALO + tile]
    z_mid = z[HALO:HALO + tile]
    yb = b_gate * (z_prev * cw_ref[0:1, :] + z_mid * cw_ref[1:2, :] + z_next * cw_ref[2:3, :])

    na = ya * lax.rsqrt(jnp.mean(ya * ya, axis=-1, keepdims=True) + EPS) * ga_ref[...]
    nb = yb * lax.rsqrt(jnp.mean(yb * yb, axis=-1, keepdims=True) + EPS) * gb_ref[...]
    mix = _dot(na.astype(bf16), w_out_ref[:WIDTH_A, :]) + _dot(nb.astype(bf16), w_out_ref[WIDTH_A:, :])

    h = _layer_norm(ALPHA * x + mix, ln1g_ref[...], ln1b_ref[...])
    h_ref[...] = h
    hp_ref[...] = _pack_rows(h)


def _mixer(x, seq_lens, w_in, seg, lnv_g, lnv_b, ws, bs_full, cw, ga, gb, w_out, ln1g, ln1b):
    n = x.shape[0]
    tile = MIX_TILE
    starts, ends, off = [], [], 0
    for s in seq_lens:
        assert s % tile == 0
        starts.append(off // tile)
        off += s
        ends.append(off // tile - 1)
    hb = tile // HALO
    n_halo_blocks = n // HALO
    full = lambda shape: pl.BlockSpec(shape, lambda i: (0,) * len(shape))
    kern = functools.partial(_mixer_kernel, tile=tile, start_tiles=tuple(starts), end_tiles=tuple(ends))
    return pl.pallas_call(
        kern,
        grid=(n // tile,),
        in_specs=[
            pl.BlockSpec((HALO, D_MODEL), lambda i: (jnp.maximum(i * hb - 1, 0), 0)),
            pl.BlockSpec((tile, D_MODEL), lambda i: (i, 0)),
            pl.BlockSpec((HALO, D_MODEL), lambda i: (jnp.minimum((i + 1) * hb, n_halo_blocks - 1), 0)),
            full(w_in.shape), full(seg.shape), full(lnv_g.shape), full(lnv_b.shape), full(ws.shape),
            full(bs_full.shape), full(cw.shape), full(ga.shape), full(gb.shape), full(w_out.shape),
            full(ln1g.shape), full(ln1b.shape),
        ],
        out_specs=[pl.BlockSpec((tile, D_MODEL), lambda i: (i, 0)),
                   pl.BlockSpec((tile, HALF), lambda i: (i, 0))],
        out_shape=[jax.ShapeDtypeStruct((n, D_MODEL), f32), jax.ShapeDtypeStruct((n, HALF), u32)],
        compiler_params=pltpu.CompilerParams(dimension_semantics=("arbitrary",),
                                             vmem_limit_bytes=VMEM_LIMIT),
        name="mixer",
    )(x, x, x, w_in, seg, lnv_g, lnv_b, ws, bs_full, cw, ga, gb, w_out, ln1g, ln1b)


def _router_kernel(h_ref, wr_hi_ref, wr_lo_ref, bias_ref, idx_ref, gate_ref, *, tile):
    h = h_ref[...]
    h_hi = h.astype(bf16)
    h_lo = (h - h_hi.astype(f32)).astype(bf16)
    nt = (((1,), (1,)), ((), ()))
    dot_nt = lambda a, b: lax.dot_general(a, b, nt, preferred_element_type=f32)
    logits = dot_nt(wr_hi_ref[...], h_hi) + dot_nt(wr_hi_ref[...], h_lo) + dot_nt(wr_lo_ref[...], h_hi)
    s = jax.nn.sigmoid(logits)
    s_sel = s + bias_ref[...]
    neg = jnp.float32(-jnp.inf)

    g3 = s_sel.reshape(N_GROUPS, PER_GROUP, tile)
    pos = lax.broadcasted_iota(i32, (N_GROUPS, PER_GROUP, tile), 1)
    m1 = jnp.max(g3, axis=1, keepdims=True)
    first = jnp.min(jnp.where(g3 == m1, pos, PER_GROUP), axis=1, keepdims=True)
    m2 = jnp.max(jnp.where(pos == first, neg, g3), axis=1, keepdims=True)
    grp = (m1 + m2).reshape(N_GROUPS, tile)

    gid = lax.broadcasted_iota(i32, (N_GROUPS, tile), 0)
    rank = jnp.zeros((N_GROUPS, tile), i32)
    for r in range(1, N_GROUPS):
        other = pltpu.roll(grp, r, 0)
        oid = pltpu.roll(gid, r, 0)
        beats = jnp.logical_or(other > grp, jnp.logical_and(other == grp, oid < gid))
        rank = rank + beats.astype(i32)
    keep = (rank < TOPK_GROUPS).reshape(N_GROUPS, 1, tile)
    cur = jnp.where(keep, g3, neg).reshape(N_EXPERTS, tile)

    eid = lax.broadcasted_iota(i32, (N_EXPERTS, tile), 0)
    idx_rows, w_rows = [], []
    for _ in range(TOP_K):
        m = jnp.max(cur, axis=0, keepdims=True)
        pick = jnp.min(jnp.where(cur == m, eid, N_EXPERTS), axis=0, keepdims=True)
        sel = eid == pick
        w_rows.append(jnp.sum(jnp.where(sel, s, 0.0), axis=0, keepdims=True))
        idx_rows.append(pick)
        cur = jnp.where(sel, neg, cur)
    w = jnp.concatenate(w_rows, axis=0)
    idx_ref[...] = jnp.concatenate(idx_rows, axis=0)
    gate_ref[...] = w / jnp.sum(w, axis=0, keepdims=True) * ROUTED_SCALE


def _router(h, wr_hi, wr_lo, bias):
    n = h.shape[0]
    tile = ROUTE_TILE
    full = lambda shape: pl.BlockSpec(shape, lambda i: (0,) * len(shape))
    return pl.pallas_call(
        functools.partial(_router_kernel, tile=tile),
        grid=(n // tile,),
        in_specs=[pl.BlockSpec((tile, D_MODEL), lambda i: (i, 0)),
                  full(wr_hi.shape), full(wr_lo.shape), full(bias.shape)],
        out_specs=[pl.BlockSpec((TOP_K, tile), lambda i: (0, i)),
                   pl.BlockSpec((TOP_K, tile), lambda i: (0, i))],
        out_shape=[jax.ShapeDtypeStruct((TOP_K, n), i32), jax.ShapeDtypeStruct((TOP_K, n), f32)],
        compiler_params=pltpu.CompilerParams(dimension_semantics=("arbitrary",),
                                             vmem_limit_bytes=VMEM_LIMIT),
        name="router",
    )(h, wr_hi, wr_lo, bias)


def _expert_kernel(be_ref, nu_ref, tok_ref, tokn_ref, dst_ref, gate_ref, wgu_ref, wd_ref, hp_hbm,
                   yk_hbm, xbuf, obuf, gsem, ssem, *, rows, n_blocks):
    b = pl.program_id(0)
    used = nu_ref[0]
    slot = b % 2

    def gather_row(idx_ref, r, s):
        return pltpu.make_async_copy(hp_hbm.at[pl.ds(idx_ref[0, 0, r], 1)],
                                     xbuf.at[s, pl.ds(r, 1)], gsem.at[s])

    def scatter_row(r, s):
        return pltpu.make_async_copy(obuf.at[s, pl.ds(r, 1)],
                                     yk_hbm.at[pl.ds(dst_ref[0, 0, r], 1)], ssem.at[s])

    def issue(make):
        def body(q, carry):
            for j in range(ISSUE_UNROLL):
                make(q * ISSUE_UNROLL + j).start()
            return carry
        lax.fori_loop(0, rows // ISSUE_UNROLL, body, 0)

    def wait_gather(s):
        pltpu.make_async_copy(hp_hbm.at[pl.ds(0, rows)], xbuf.at[s], gsem.at[s]).wait()

    def wait_scatter(s):
        pltpu.make_async_copy(obuf.at[s], yk_hbm.at[pl.ds(0, rows)], ssem.at[s]).wait()

    @pl.when(jnp.logical_and(b == 0, used > 0))
    def _():
        issue(lambda r: gather_row(tok_ref, r, 0))

    @pl.when(b + 1 < used)
    def _():
        issue(lambda r: gather_row(tokn_ref, r, 1 - slot))

    @pl.when(b < used)
    def _():
        wait_gather(slot)
        w = xbuf[slot]
        x_hi = _unpack_hi(w).astype(bf16)
        x_lo = _unpack_lo(w).astype(bf16)
        gu = _dot(x_hi, wgu_ref[0, :HALF, :]) + _dot(x_lo, wgu_ref[0, HALF:, :])
        a = gu[:, :D_EXPERT]
        act = (a * jax.nn.sigmoid(a) * gu[:, D_EXPERT:]).astype(bf16)
        o = _dot(act, wd_ref[0]) * gate_ref[...]

        @pl.when(b >= 2)
        def _():
            wait_scatter(slot)

        obuf[slot] = _pack_rows(o)
        issue(lambda r: scatter_row(r, slot))

    @pl.when(b == n_blocks - 1)
    def _():
        @pl.when(used >= 2)
        def _():
            wait_scatter(used % 2)

        @pl.when(used >= 1)
        def _():
            wait_scatter((used - 1) % 2)

        spill = pltpu.make_async_copy(obuf.at[0], yk_hbm.at[pl.ds(yk_hbm.shape[0] - rows, rows)],
                                      ssem.at[0])
        spill.start()
        spill.wait()


def _experts(block_e, n_used, slot_tok, slot_dst, slot_gate, wgu, wd, hp, n_out_rows):
    rows = EXPERT_ROWS
    n_blocks = slot_tok.shape[0]
    grid_spec = pltpu.PrefetchScalarGridSpec(
        num_scalar_prefetch=2,
        grid=(n_blocks,),
        in_specs=[
            pl.BlockSpec((1, 1, rows), lambda b, be, nu: (b, 0, 0), memory_space=pltpu.SMEM),
            pl.BlockSpec((1, 1, rows), lambda b, be, nu: (jnp.minimum(b + 1, n_blocks - 1), 0, 0),
                         memory_space=pltpu.SMEM),
            pl.BlockSpec((1, 1, rows), lambda b, be, nu: (b, 0, 0), memory_space=pltpu.SMEM),
            pl.BlockSpec((rows, 1), lambda b, be, nu: (b, 0)),
            pl.BlockSpec((1, D_MODEL, 2 * D_EXPERT), lambda b, be, nu: (be[b], 0, 0)),
            pl.BlockSpec((1, D_EXPERT, D_MODEL), lambda b, be, nu: (be[b], 0, 0)),
            pl.BlockSpec(memory_space=pl.ANY),
        ],
        out_specs=pl.BlockSpec(memory_space=pl.ANY),
        scratch_shapes=[
            pltpu.VMEM((2, rows, HALF), u32),
            pltpu.VMEM((2, rows, HALF), u32),
            pltpu.SemaphoreType.DMA((2,)),
            pltpu.SemaphoreType.DMA((2,)),
        ],
    )
    return pl.pallas_call(
        functools.partial(_expert_kernel, rows=rows, n_blocks=n_blocks),
        grid_spec=grid_spec,
        out_shape=jax.ShapeDtypeStruct((n_out_rows, HALF), u32),
        compiler_params=pltpu.CompilerParams(dimension_semantics=("arbitrary",),
                                             vmem_limit_bytes=VMEM_LIMIT),
        name="experts",
    )(block_e, n_used, slot_tok, slot_tok, slot_dst, slot_gate, wgu, wd, hp)


def _group_by_expert(idx_t, gate_t, n):
    rows = EXPERT_ROWS
    nk = TOP_K * n
    p_total = nk + N_EXPERTS * rows
    n_blocks = p_total // rows
    flat_e = idx_t.reshape(-1)
    se, order = lax.sort_key_val(flat_e, jnp.arange(nk, dtype=i32))
    bounds = jnp.searchsorted(se, jnp.arange(N_EXPERTS + 1, dtype=i32), side="left").astype(i32)
    counts = bounds[1:] - bounds[:-1]
    start_sorted = bounds[:-1]
    padded = (counts + rows - 1) // rows * rows
    pad_end = jnp.cumsum(padded).astype(i32)
    start_pad = pad_end - padded
    block_e = jnp.minimum(
        jnp.searchsorted(pad_end, jnp.arange(n_blocks, dtype=i32) * rows, side="right"),
        N_EXPERTS - 1).astype(i32)
    n_used = (pad_end[-1] // rows).reshape(1)
    p = jnp.arange(p_total, dtype=i32)
    e_p = jnp.repeat(block_e, rows)
    off = p - start_pad[e_p]
    valid = off < counts[e_p]
    a_p = order[jnp.clip(start_sorted[e_p] + off, 0, nk - 1)]
    slot_tok = jnp.where(valid, a_p % n, 0)
    slot_dst = jnp.where(valid, a_p, nk + p % rows)
    slot_gate = jnp.where(valid, gate_t.reshape(-1)[a_p], 0.0)
    return (block_e, n_used, slot_tok.reshape(n_blocks, 1, rows), slot_dst.reshape(n_blocks, 1, rows),
            slot_gate.reshape(p_total, 1))


def _final_kernel(h_ref, *refs):
    yk_refs = refs[:TOP_K]
    wsgu_ref, wsd_ref, g_ref, b_ref, y_ref = refs[TOP_K:]
    h = h_ref[...]
    hi = _unpack_hi(yk_refs[0][...])
    lo = _unpack_lo(yk_refs[0][...])
    for r in yk_refs[1:]:
        hi = hi + _unpack_hi(r[...])
        lo = lo + _unpack_lo(r[...])
    gu = _dot(h.astype(bf16), wsgu_ref[...])
    a = gu[:, :D_EXPERT]
    act = (a * jax.nn.sigmoid(a) * gu[:, D_EXPERT:]).astype(bf16)
    moe = jnp.concatenate([hi, lo], axis=1) + _dot(act, wsd_ref[...])
    y_ref[...] = _layer_norm(ALPHA * h + moe, g_ref[...], b_ref[...])


def _final(h, yk, wsgu, wsd, g, b):
    n = h.shape[0]
    tile = FINAL_TILE
    nt = n // tile
    full = lambda shape: pl.BlockSpec(shape, lambda i: (0,) * len(shape))
    yk_specs = [pl.BlockSpec((tile, HALF), functools.partial(lambda i, k: (k * nt + i, 0), k=k))
                for k in range(TOP_K)]
    return pl.pallas_call(
        _final_kernel,
        grid=(nt,),
        in_specs=[pl.BlockSpec((tile, D_MODEL), lambda i: (i, 0))] + yk_specs
        + [full(wsgu.shape), full(wsd.shape), full(g.shape), full(b.shape)],
        out_specs=pl.BlockSpec((tile, D_MODEL), lambda i: (i, 0)),
        out_shape=jax.ShapeDtypeStruct((n, D_MODEL), f32),
        compiler_params=pltpu.CompilerParams(dimension_semantics=("arbitrary",),
                                             vmem_limit_bytes=VMEM_LIMIT),
        name="final",
    )(h, *([yk] * TOP_K), wsgu, wsd, g, b)


def _layer(xs, w_in, ln_v_g, ln_v_b, w_spatial, b_spatial, conv_w, g_out_a, g_out_b, w_out, ln1_g,
           ln1_b, w_router, router_bias, w_gate_up, w_down, w_shared_gate_up, w_shared_down, ln2_g,
           ln2_b):
    seq_lens = [s for x in xs for s in [x.shape[1]] * x.shape[0]]
    x = jnp.concatenate([x.reshape(-1, D_MODEL) for x in xs], axis=0)
    n = x.shape[0]
    row = lambda v: v.reshape(1, -1).astype(f32)

    head = jnp.arange(WIDTH_A, dtype=i32) // HEAD_DIM
    seg = (head[:, None] == head[None, :]).astype(bf16)
    bs_full = jnp.repeat(b_spatial.T, HEAD_DIM, axis=1)
    h, hp = _mixer(x, seq_lens, w_in.astype(bf16), seg, row(ln_v_g), row(ln_v_b),
                   w_spatial.astype(bf16), bs_full, conv_w.T, row(g_out_a), row(g_out_b),
                   w_out.astype(bf16), row(ln1_g), row(ln1_b))

    wr_t = w_router.T.astype(f32)
    wr_hi = wr_t.astype(bf16)
    wr_lo = (wr_t - wr_hi.astype(f32)).astype(bf16)
    idx_t, gate_t = _router(h, wr_hi, wr_lo, router_bias.reshape(-1, 1).astype(f32))

    block_e, n_used, slot_tok, slot_dst, slot_gate = _group_by_expert(idx_t, gate_t, n)
    yk = _experts(block_e, n_used, slot_tok, slot_dst, slot_gate, w_gate_up.astype(bf16),
                  w_down.astype(bf16), hp, TOP_K * n + EXPERT_ROWS)

    y = _final(h, yk, w_shared_gate_up.astype(bf16), w_shared_down.astype(bf16), row(ln2_g), row(ln2_b))
    outs, off = [], 0
    for xi in xs:
        cnt = xi.shape[0] * xi.shape[1]
        outs.append(y[off:off + cnt].reshape(xi.shape))
        off += cnt
    return tuple(outs)


def kernel(x_prompt, x_sample, w_in, ln_v_g, ln_v_b, w_spatial, b_spatial, conv_w, g_out_a, g_out_b, w_out, ln1_g, ln1_b, w_router, router_bias, w_gate_up, w_down, w_shared_gate_up, w_shared_down, ln2_g, ln2_b):
    assert w_in.shape[0] == 1
    return _layer((x_prompt, x_sample), w_in[0], ln_v_g[0], ln_v_b[0], w_spatial[0], b_spatial[0],
                  conv_w[0], g_out_a[0], g_out_b[0], w_out[0], ln1_g[0], ln1_b[0], w_router[0],
                  router_bias[0], w_gate_up[0], w_down[0], w_shared_gate_up[0], w_shared_down[0],
                  ln2_g[0], ln2_b[0])
```
